```python
import jax, jax.numpy as jnp
from jax import lax
import numpy as np

D_MODEL = 4096
BATCH = 2
SEQ = 4096
DEPTH = 1

CHUNK = 64
N_LEFT_CHUNKS = 8
BAND = (N_LEFT_CHUNKS + 1) * CHUNK
MIX_WIDTH = D_MODEL
ATT_WIDTH = MIX_WIDTH // 2
RNN_WIDTH = MIX_WIDTH - ATT_WIDTH
HEAD_DIM = 128
N_ATT_HEADS = ATT_WIDTH // HEAD_DIM
N_RNN_BLOCKS = 16
RNN_BLOCK = RNN_WIDTH // N_RNN_BLOCKS
REL_CLIP = 128
N_REL = 2 * REL_CLIP + 1
CONV_WIDTH = 4
LRU_C = 8.0
D_FF = ((8 * D_MODEL // 3 + 255) // 256) * 256
PLE_DIM = 256
IN_PROJ_WIDTH = 3 * ATT_WIDTH + 2 * RNN_WIDTH
EPS = 1e-6
NEG_INF = -1e30

kernel_name = "hybrid_chunk_attn_rglru_macaron_layer"


def rms_norm(x, g):
    x32 = x.astype(jnp.float32)
    y = x32 * lax.rsqrt(jnp.mean(x32 * x32, axis=-1, keepdims=True) + EPS)
    return (y * g.astype(jnp.float32)).astype(x.dtype)


def swiglu(x, w_gate, w_up, w_down):
    return (jax.nn.silu(x @ w_gate) * (x @ w_up)) @ w_down


def chunked_rel_attention(q, k, v, rel_bias):
    B, S, H, Dh = q.shape
    n_chunks = S // CHUNK
    pad = N_LEFT_CHUNKS * CHUNK
    k_pad = jnp.pad(k, ((0, 0), (pad, 0), (0, 0), (0, 0)))
    v_pad = jnp.pad(v, ((0, 0), (pad, 0), (0, 0), (0, 0)))
    q_off = jnp.arange(CHUNK)[:, None]
    k_off = jnp.arange(BAND)[None, :] - pad
    rel_idx = jnp.clip(q_off - k_off, -REL_CLIP, REL_CLIP) + REL_CLIP
    bias = rel_bias.astype(jnp.float32)[:, rel_idx]
    scale = HEAD_DIM ** -0.5

    def one_chunk(c):
        start = c * CHUNK
        q_c = lax.dynamic_slice_in_dim(q, start, CHUNK, axis=1)
        k_c = lax.dynamic_slice_in_dim(k_pad, start, BAND, axis=1)
        v_c = lax.dynamic_slice_in_dim(v_pad, start, BAND, axis=1)
        s = jnp.einsum('bqhd,bkhd->bhqk', q_c, k_c).astype(jnp.float32) * scale + bias
        valid = (start + k_off) >= 0
        s = jnp.where(valid[None, None], s, NEG_INF)
        w = jax.nn.softmax(s, axis=-1).astype(v.dtype)
        return jnp.einsum('bhqk,bkhd->bqhd', w, v_c)

    out = lax.map(one_chunk, jnp.arange(n_chunks))
    return out.transpose(1, 0, 2, 3, 4).reshape(B, S, H * Dh)


def causal_depthwise_conv(x, w, b):
    S = x.shape[1]
    xp = jnp.pad(x, ((0, 0), (CONV_WIDTH - 1, 0), (0, 0)))
    out = b + xp[:, 0:S] * w[0]
    for j in range(1, CONV_WIDTH):
        out = out + xp[:, j:j + S] * w[j]
    return out


def rg_lru(x, w_a, b_a, w_i, b_i, lam):
    B, S, C = x.shape
    xb = x.reshape(B, S, N_RNN_BLOCKS, RNN_BLOCK)
    r = jax.nn.sigmoid(jnp.einsum('bsnc,ncd->bsnd', xb, w_a).reshape(B, S, C) + b_a)
    i = jax.nn.sigmoid(jnp.einsum('bsnc,ncd->bsnd', xb, w_i).reshape(B, S, C) + b_i)
    log_a = -LRU_C * r.astype(jnp.float32) * jax.nn.softplus(-lam.astype(jnp.float32))
    a = jnp.exp(log_a)
    gated_x = jnp.sqrt(-jnp.expm1(2.0 * log_a)) * (i * x).astype(jnp.float32)

    def combine(left, right):
        a1, b1 = left
        a2, b2 = right
        return a1 * a2, a2 * b1 + b2

    _, h = lax.associative_scan(combine, (a, gated_x), axis=1)
    return h.astype(x.dtype)


def setup_inputs(seed: int = 0) -> dict:
    key = jax.random.key(seed)
    ks = jax.random.split(key, 32)

    def dense(k, shape, fan_in):
        return jax.random.normal(k, shape, jnp.float32) * (fan_in ** -0.5)

    def gain(k):
        return 1.0 + 0.05 * jax.random.normal(k, (DEPTH, D_MODEL), jnp.float32)

    u = jax.random.uniform(ks[20], (DEPTH, RNN_WIDTH), jnp.float32, 0.9, 0.999)
    return {
        "x": jax.random.normal(ks[0], (BATCH, SEQ, D_MODEL), jnp.float32),
        "p": jax.random.normal(ks[1], (DEPTH, BATCH, SEQ, PLE_DIM), jnp.float32),
        "ffn1_pre_g": gain(ks[2]),
        "ffn1_w_gate": dense(ks[3], (DEPTH, D_MODEL, D_FF), D_MODEL),
        "ffn1_w_up": dense(ks[4], (DEPTH, D_MODEL, D_FF), D_MODEL),
        "ffn1_w_down": dense(ks[5], (DEPTH, D_FF, D_MODEL), D_FF),
        "ffn1_post_g": gain(ks[6]),
        "mix_pre_g": gain(ks[7]),
        "w_in": dense(ks[8], (DEPTH, D_MODEL, IN_PROJ_WIDTH), D_MODEL),
        "rel_bias": 0.5 * jax.random.normal(ks[9], (DEPTH, N_ATT_HEADS, N_REL), jnp.float32),
        "conv_w": dense(ks[10], (DEPTH, CONV_WIDTH, RNN_WIDTH), CONV_WIDTH),
        "conv_b": 0.02 * jax.random.normal(ks[11], (DEPTH, RNN_WIDTH), jnp.float32),
        "lru_w_a": dense(ks[12], (DEPTH, N_RNN_BLOCKS, RNN_BLOCK, RNN_BLOCK), RNN_BLOCK),
        "lru_b_a": 0.02 * jax.random.normal(ks[13], (DEPTH, RNN_WIDTH), jnp.float32),
        "lru_w_i": dense(ks[14], (DEPTH, N_RNN_BLOCKS, RNN_BLOCK, RNN_BLOCK), RNN_BLOCK),
        "lru_b_i": 0.02 * jax.random.normal(ks[15], (DEPTH, RNN_WIDTH), jnp.float32),
        "lru_lambda": jnp.log(u) - jnp.log1p(-u),
        "w_out": dense(ks[16], (DEPTH, MIX_WIDTH, D_MODEL), MIX_WIDTH),
        "mix_post_g": gain(ks[17]),
        "ffn2_pre_g": gain(ks[18]),
        "ffn2_w_gate": dense(ks[19], (DEPTH, D_MODEL, D_FF), D_MODEL),
        "ffn2_w_up": dense(ks[21], (DEPTH, D_MODEL, D_FF), D_MODEL),
        "ffn2_w_down": dense(ks[22], (DEPTH, D_FF, D_MODEL), D_FF),
        "ffn2_post_g": gain(ks[23]),
        "ple_w_proj": dense(ks[24], (DEPTH, PLE_DIM, D_MODEL), PLE_DIM),
        "ple_w_gate": dense(ks[25], (DEPTH, D_MODEL, D_MODEL), D_MODEL),
        "ple_post_g": gain(ks[26]),
    }


def reference(x, p, ffn1_pre_g, ffn1_w_gate, ffn1_w_up, ffn1_w_down, ffn1_post_g,
              mix_pre_g, w_in, rel_bias, conv_w, conv_b, lru_w_a, lru_b_a, lru_w_i,
              lru_b_i, lru_lambda, w_out, mix_post_g, ffn2_pre_g, ffn2_w_gate,
              ffn2_w_up, ffn2_w_down, ffn2_post_g, ple_w_proj, ple_w_gate, ple_post_g):
    B, S, _ = x.shape
    h = x
    for l in range(DEPTH):
        f = swiglu(rms_norm(h, ffn1_pre_g[l]), ffn1_w_gate[l], ffn1_w_up[l], ffn1_w_down[l])
        h = h + 0.5 * rms_norm(f, ffn1_post_g[l])

        u = rms_norm(h, mix_pre_g[l])
        z = u @ w_in[l]
        q, k, v, xr, yg = jnp.split(
            z, [ATT_WIDTH, 2 * ATT_WIDTH, 3 * ATT_WIDTH, 3 * ATT_WIDTH + RNN_WIDTH], axis=-1)
        heads = (B, S, N_ATT_HEADS, HEAD_DIM)
        att = chunked_rel_attention(q.reshape(heads), k.reshape(heads), v.reshape(heads),
                                    rel_bias[l])
        xr = causal_depthwise_conv(xr, conv_w[l], conv_b[l])
        rec = rg_lru(xr, lru_w_a[l], lru_b_a[l], lru_w_i[l], lru_b_i[l], lru_lambda[l])
        rec = rec * jax.nn.gelu(yg)
        mix = jnp.concatenate([att, rec], axis=-1) @ w_out[l]
        h = h + rms_norm(mix, mix_post_g[l])

        f = swiglu(rms_norm(h, ffn2_pre_g[l]), ffn2_w_gate[l], ffn2_w_up[l], ffn2_w_down[l])
        h = h + 0.5 * rms_norm(f, ffn2_post_g[l])

        e = (p[l] @ ple_w_proj[l]) * jax.nn.sigmoid(h @ ple_w_gate[l])
        h = h + rms_norm(e, ple_post_g[l])
    return h
```

```python
import functools

import numpy as np
import jax
import jax.numpy as jnp
from jax import lax
from jax.experimental import pallas as pl
from jax.experimental.pallas import tpu as pltpu

CHUNK = 64
N_LEFT_CHUNKS = 8
PAD = N_LEFT_CHUNKS * CHUNK
BAND = PAD + CHUNK
HEAD_DIM = 128
REL_CLIP = 128
CONV_WIDTH = 4
LRU_C = 8.0
EPS = 1e-6
NEG_INF = -1e30
LANES = 128
SUBLANES = 8
V7X_VMEM_BYTES = 64 * 1024 * 1024
VMEM_CAP_BYTES = 60000 * 1024
BF16 = jnp.bfloat16
F32 = jnp.float32


def _nbytes(shape, dtype):
    return int(np.prod(shape)) * jnp.dtype(dtype).itemsize


def _params(blocks, extra_bytes=0):
    need = 2 * sum(_nbytes(s, d) for s, d in blocks) + extra_bytes + (2 << 20)
    assert need <= VMEM_CAP_BYTES, (need, VMEM_CAP_BYTES)
    return pltpu.CompilerParams(vmem_limit_bytes=min(VMEM_CAP_BYTES, max(need, 16 << 20)))


def _rms(x, g):
    return x * lax.rsqrt(jnp.mean(x * x, axis=-1, keepdims=True) + EPS) * g


def _norm_cast_kernel(x_ref, g_ref, o_ref):
    o_ref[...] = _rms(x_ref[...], g_ref[...]).astype(o_ref.dtype)


def _norm_cast(x, g, tm=256):
    m, d = x.shape
    blocks = [((tm, d), F32), ((1, d), F32), ((tm, d), BF16)]
    return pl.pallas_call(
        _norm_cast_kernel,
        grid=(m // tm,),
        in_specs=[pl.BlockSpec((tm, d), lambda i: (i, 0)), pl.BlockSpec((1, d), lambda i: (0, 0))],
        out_specs=pl.BlockSpec((tm, d), lambda i: (i, 0)),
        out_shape=jax.ShapeDtypeStruct((m, d), BF16),
        compiler_params=_params(blocks, 2 * _nbytes((tm, d), F32)),
        name="norm_cast",
    )(x, g.reshape(1, d))


def _residual_kernel(*refs, scale, next_mode):
    if next_mode == "norm":
        h_ref, f_ref, gp_ref, gn_ref, ho_ref, uo_ref = refs
    elif next_mode == "cast":
        h_ref, f_ref, gp_ref, ho_ref, uo_ref = refs
    else:
        h_ref, f_ref, gp_ref, ho_ref = refs
    h = h_ref[...] + scale * _rms(f_ref[...], gp_ref[...])
    ho_ref[...] = h
    if next_mode == "norm":
        uo_ref[...] = _rms(h, gn_ref[...]).astype(uo_ref.dtype)
    elif next_mode == "cast":
        uo_ref[...] = h.astype(uo_ref.dtype)


def _residual(h, f, g_post, g_next, *, scale, next_mode, tm=256):
    m, d = h.shape
    row = pl.BlockSpec((tm, d), lambda i: (i, 0))
    vec = pl.BlockSpec((1, d), lambda i: (0, 0))
    args, in_specs = [h, f, g_post.reshape(1, d)], [row, row, vec]
    out_shape, out_specs = [jax.ShapeDtypeStruct((m, d), F32)], [row]
    blocks = [((tm, d), F32)] * 3
    if next_mode == "norm":
        args.append(g_next.reshape(1, d))
        in_specs.append(vec)
    if next_mode is not None:
        out_shape.append(jax.ShapeDtypeStruct((m, d), BF16))
        out_specs.append(row)
        blocks.append(((tm, d), BF16))
    out = pl.pallas_call(
        functools.partial(_residual_kernel, scale=scale, next_mode=next_mode),
        grid=(m // tm,),
        in_specs=in_specs, out_specs=out_specs, out_shape=out_shape,
        compiler_params=_params(blocks, 3 * _nbytes((tm, d), F32)),
        name="residual_" + str(next_mode),
    )(*args)
    return tuple(out) if next_mode is not None else (out[0], None)


def _gate_up_kernel(u_ref, wg_ref, wu_ref, o_ref):
    u = u_ref[...]
    g = jnp.dot(u, wg_ref[...], preferred_element_type=F32)
    v = jnp.dot(u, wu_ref[...], preferred_element_type=F32)
    o_ref[...] = (jax.nn.silu(g) * v).astype(o_ref.dtype)


def _gate_up(u, wg, wu, tm=1024, tn=256):
    m, k = u.shape
    n = wg.shape[1]
    blocks = [((tm, k), BF16), ((k, tn), BF16), ((k, tn), BF16), ((tm, tn), BF16)]
    return pl.pallas_call(
        _gate_up_kernel,
        grid=(m // tm, n // tn),
        in_specs=[pl.BlockSpec((tm, k), lambda i, j: (i, 0)),
                  pl.BlockSpec((k, tn), lambda i, j: (0, j)),
                  pl.BlockSpec((k, tn), lambda i, j: (0, j))],
        out_specs=pl.BlockSpec((tm, tn), lambda i, j: (i, j)),
        out_shape=jax.ShapeDtypeStruct((m, n), BF16),
        compiler_params=_params(blocks, 4 * _nbytes((tm, tn), F32)),
        name="ffn_gate_up",
    )(u, wg, wu)


def _matmul_kernel(a_ref, w_ref, o_ref):
    o_ref[...] = jnp.dot(a_ref[...], w_ref[...], preferred_element_type=F32).astype(o_ref.dtype)


def _matmul(a, w, *, out_dtype, tm, tn, col0=0, n=None, name):
    m, k = a.shape
    n = w.shape[1] if n is None else n
    assert col0 % tn == 0 and n % tn == 0 and m % tm == 0
    jb = col0 // tn
    blocks = [((tm, k), a.dtype), ((k, tn), w.dtype), ((tm, tn), out_dtype)]
    return pl.pallas_call(
        _matmul_kernel,
        grid=(m // tm, n // tn),
        in_specs=[pl.BlockSpec((tm, k), lambda i, j: (i, 0)),
                  pl.BlockSpec((k, tn), lambda i, j: (0, j + jb))],
        out_specs=pl.BlockSpec((tm, tn), lambda i, j: (i, j)),
        out_shape=jax.ShapeDtypeStruct((m, n), out_dtype),
        compiler_params=_params(blocks, 2 * _nbytes((tm, tn), F32)),
        name=name,
    )(a, w)


def _mix_out_kernel(att_ref, rec_ref, wa_ref, wr_ref, o_ref):
    acc = jnp.dot(att_ref[...], wa_ref[...], preferred_element_type=F32)
    acc += jnp.dot(rec_ref[...], wr_ref[...], preferred_element_type=F32)
    o_ref[...] = acc


def _mix_out(att, rec, w_out, tm=1024, tn=512):
    m, ka = att.shape
    kr = rec.shape[1]
    n = w_out.shape[1]
    assert ka % kr == 0
    blocks = [((tm, ka), BF16), ((tm, kr), BF16), ((ka, tn), BF16), ((kr, tn), BF16), ((tm, tn), F32)]
    return pl.pallas_call(
        _mix_out_kernel,
        grid=(m // tm, n // tn),
        in_specs=[pl.BlockSpec((tm, ka), lambda i, j: (i, 0)),
                  pl.BlockSpec((tm, kr), lambda i, j: (i, 0)),
                  pl.BlockSpec((ka, tn), lambda i, j: (0, j)),
                  pl.BlockSpec((kr, tn), lambda i, j: (ka // kr, j))],
        out_specs=pl.BlockSpec((tm, tn), lambda i, j: (i, j)),
        out_shape=jax.ShapeDtypeStruct((m, n), F32),
        compiler_params=_params(blocks, 2 * _nbytes((tm, tn), F32)),
        name="mix_out",
    )(att, rec, w_out, w_out)


def _ple_kernel(p_ref, h_ref, wp_ref, wg_ref, o_ref):
    e = jnp.dot(p_ref[...], wp_ref[...], preferred_element_type=F32)
    gate = jnp.dot(h_ref[...], wg_ref[...], preferred_element_type=F32)
    o_ref[...] = e * jax.nn.sigmoid(gate)


def _ple(p, hb, wp, wg, tm=1024, tn=512):
    m, kp = p.shape
    k = hb.shape[1]
    n = wg.shape[1]
    blocks = [((tm, kp), BF16), ((tm, k), BF16), ((kp, tn), BF16), ((k, tn), BF16), ((tm, tn), F32)]
    return pl.pallas_call(
        _ple_kernel,
        grid=(m // tm, n // tn),
        in_specs=[pl.BlockSpec((tm, kp), lambda i, j: (i, 0)),
                  pl.BlockSpec((tm, k), lambda i, j: (i, 0)),
                  pl.BlockSpec((kp, tn), lambda i, j: (0, j)),
                  pl.BlockSpec((k, tn), lambda i, j: (0, j))],
        out_specs=pl.BlockSpec((tm, tn), lambda i, j: (i, j)),
        out_shape=jax.ShapeDtypeStruct((m, n), F32),
        compiler_params=_params(blocks, 4 * _nbytes((tm, tn), F32)),
        name="ple",
    )(p, hb, wp, wg)


def _attn_kernel(q_ref, k_ref, v_ref, bias_ref, o_ref, kpad_ref, vpad_ref):
    seq = q_ref.shape[0]
    zeros = jnp.zeros((PAD, HEAD_DIM), kpad_ref.dtype)
    kpad_ref[0:PAD, :] = zeros
    vpad_ref[0:PAD, :] = zeros
    kpad_ref[PAD:PAD + seq, :] = k_ref[...]
    vpad_ref[PAD:PAD + seq, :] = v_ref[...]
    bias = bias_ref[0]
    col = lax.broadcasted_iota(jnp.int32, (CHUNK, BAND), 1)
    scale = HEAD_DIM ** -0.5

    def chunk(c, carry):
        start = pl.multiple_of(c * CHUNK, CHUNK)
        qc = q_ref[pl.ds(start, CHUNK), :]
        kb = kpad_ref[pl.ds(start, BAND), :]
        vb = vpad_ref[pl.ds(start, BAND), :]
        s = lax.dot_general(qc, kb, (((1,), (1,)), ((), ())), preferred_element_type=F32)
        s = s * scale + bias
        s = jnp.where(col >= PAD - start, s, NEG_INF)
        p = jnp.exp(s - jnp.max(s, axis=-1, keepdims=True))
        denom = jnp.sum(p, axis=-1, keepdims=True)
        o = jnp.dot(p.astype(vb.dtype), vb, preferred_element_type=F32) / denom
        o_ref[pl.ds(start, CHUNK), :] = o.astype(o_ref.dtype)
        return carry

    lax.fori_loop(0, seq // CHUNK, chunk, 0, unroll=2)


def _attention(qkv, bias, batch, seq, n_heads):
    m = qkv.shape[0]
    blk = (seq, HEAD_DIM)
    blocks = [(blk, BF16)] * 4 + [((1, CHUNK, BAND), F32)]
    return pl.pallas_call(
        _attn_kernel,
        grid=(batch, n_heads),
        in_specs=[pl.BlockSpec(blk, lambda b, h: (b, h)),
                  pl.BlockSpec(blk, lambda b, h: (b, n_heads + h)),
                  pl.BlockSpec(blk, lambda b, h: (b, 2 * n_heads + h)),
                  pl.BlockSpec((1, CHUNK, BAND), lambda b, h: (h, 0, 0))],
        out_specs=pl.BlockSpec(blk, lambda b, h: (b, h)),
        out_shape=jax.ShapeDtypeStruct((m, n_heads * HEAD_DIM), BF16),
        scratch_shapes=[pltpu.VMEM((PAD + seq, HEAD_DIM), BF16), pltpu.VMEM((PAD + seq, HEAD_DIM), BF16)],
        compiler_params=_params(blocks, 2 * _nbytes((PAD + seq, HEAD_DIM), BF16) + (8 << 20)),
        name="chunk_attention",
    )(qkv, qkv, qkv, bias)


def _lru_kernel(xr_ref, yg_ref, cw_ref, cb_ref, wa_ref, wi_ref, ba_ref, bi_ref, lam_ref, o_ref,
                xs_ref, h_ref):
    t = pl.program_id(2)
    tt = xr_ref.shape[0]

    @pl.when(t == 0)
    def _():
        xs_ref[0:SUBLANES, :] = jnp.zeros((SUBLANES, LANES), F32)
        h_ref[...] = jnp.zeros_like(h_ref)

    @pl.when(t > 0)
    def _():
        xs_ref[0:SUBLANES, :] = xs_ref[tt:tt + SUBLANES, :]

    xs_ref[SUBLANES:SUBLANES + tt, :] = xr_ref[...]
    xc = cb_ref[...] + xs_ref[pl.ds(SUBLANES - 3, tt), :] * cw_ref[0:1, :]
    for j in range(1, CONV_WIDTH):
        xc = xc + xs_ref[pl.ds(SUBLANES - 3 + j, tt), :] * cw_ref[j:j + 1, :]

    xb = xc.astype(BF16)
    r = jax.nn.sigmoid(jnp.dot(xb, wa_ref[0], preferred_element_type=F32) + ba_ref[...])
    gi = jax.nn.sigmoid(jnp.dot(xb, wi_ref[0], preferred_element_type=F32) + bi_ref[...])
    lam = lam_ref[...]
    softplus_neg_lam = jnp.maximum(-lam, 0.0) + jnp.log1p(jnp.exp(-jnp.abs(lam)))
    log_a = -LRU_C * r * softplus_neg_lam
    a = jnp.exp(log_a)
    th = jnp.tanh(log_a)
    b = jnp.sqrt(-2.0 * th / (1.0 - th)) * (gi * xc)

    row = lax.broadcasted_iota(jnp.int32, (tt, LANES), 0) & (SUBLANES - 1)
    for s in (1, 2, 4):
        keep = row >= s
        a_prev = pltpu.roll(a, s, 0)
        b_prev = pltpu.roll(b, s, 0)
        b = jnp.where(keep, a * b_prev + b, b)
        a = jnp.where(keep, a * a_prev, a)

    gate = jax.nn.gelu(yg_ref[...])
    h = h_ref[0:1, :]
    for g in range(tt // SUBLANES):
        rows = slice(g * SUBLANES, (g + 1) * SUBLANES)
        hg = a[rows] * h + b[rows]
        o_ref[rows, :] = (hg * gate[rows]).astype(o_ref.dtype)
        h = hg[SUBLANES - 1:SUBLANES, :]
    h_ref[0:1, :] = h


def _lru_branch(xy, conv_w, conv_b, wa, wi, ba, bi, lam, batch, seq, tt=512):
    m = xy.shape[0]
    c = conv_w.shape[1]
    nb = c // LANES
    tb = seq // tt
    blk = (tt, LANES)
    vec = pl.BlockSpec((1, LANES), lambda b, n, t: (0, n))
    sq = pl.BlockSpec((1, LANES, LANES), lambda b, n, t: (n, 0, 0))
    blocks = [(blk, F32)] * 2 + [(blk, BF16)] + [((LANES, LANES), BF16)] * 2
    return pl.pallas_call(
        _lru_kernel,
        grid=(batch, nb, tb),
        in_specs=[pl.BlockSpec(blk, lambda b, n, t: (b * tb + t, n)),
                  pl.BlockSpec(blk, lambda b, n, t: (b * tb + t, nb + n)),
                  pl.BlockSpec((CONV_WIDTH, LANES), lambda b, n, t: (0, n)),
                  vec, sq, sq, vec, vec, vec],
        out_specs=pl.BlockSpec(blk, lambda b, n, t: (b * tb + t, n)),
        out_shape=jax.ShapeDtypeStruct((m, c), BF16),
        scratch_shapes=[pltpu.VMEM((tt + SUBLANES, LANES), F32), pltpu.VMEM((SUBLANES, LANES), F32)],
        compiler_params=_params(blocks, 16 * _nbytes(blk, F32)),
        name="conv_rglru",
    )(xy, xy, conv_w, conv_b.reshape(1, c), wa, wi, ba.reshape(1, c), bi.reshape(1, c), lam.reshape(1, c))


def _rel_bias_table(rel_bias):
    q_off = np.arange(CHUNK)[:, None]
    k_off = np.arange(BAND)[None, :] - PAD
    idx = np.clip(q_off - k_off, -REL_CLIP, REL_CLIP) + REL_CLIP
    return rel_bias.astype(F32)[:, idx]


def _ffn(h, u, w_gate, w_up, w_down, g_post, g_next, next_mode):
    act = _gate_up(u, w_gate.astype(BF16), w_up.astype(BF16))
    f = _matmul(act, w_down.astype(BF16), out_dtype=F32, tm=512, tn=512, name="ffn_down")
    return _residual(h, f, g_post, g_next, scale=0.5, next_mode=next_mode)


def kernel(x, p, ffn1_pre_g, ffn1_w_gate, ffn1_w_up, ffn1_w_down, ffn1_post_g, mix_pre_g, w_in, rel_bias, conv_w, conv_b, lru_w_a, lru_b_a, lru_w_i, lru_b_i, lru_lambda, w_out, mix_post_g, ffn2_pre_g, ffn2_w_gate, ffn2_w_up, ffn2_w_down, ffn2_post_g, ple_w_proj, ple_w_gate, ple_post_g):
    batch, seq, d = x.shape
    depth = p.shape[0]
    m = batch * seq
    n_heads = rel_bias.shape[1]
    att_w = n_heads * HEAD_DIM
    rnn_w = conv_w.shape[2]
    h = x.reshape(m, d)
    u = _norm_cast(h, ffn1_pre_g[0])
    for l in range(depth):
        h, u = _ffn(h, u, ffn1_w_gate[l], ffn1_w_up[l], ffn1_w_down[l], ffn1_post_g[l],
                    mix_pre_g[l], "norm")

        w_in_b = w_in[l].astype(BF16)
        qkv = _matmul(u, w_in_b, out_dtype=BF16, tm=1024, tn=512, col0=0, n=3 * att_w, name="in_proj_qkv")
        xy = _matmul(u, w_in_b, out_dtype=F32, tm=1024, tn=512, col0=3 * att_w, n=2 * rnn_w,
                     name="in_proj_rnn")
        att = _attention(qkv, _rel_bias_table(rel_bias[l]), batch, seq, n_heads)
        rec = _lru_branch(xy, conv_w[l], conv_b[l], lru_w_a[l].astype(BF16), lru_w_i[l].astype(BF16),
                          lru_b_a[l], lru_b_i[l], lru_lambda[l], batch, seq)
        mix = _mix_out(att, rec, w_out[l].astype(BF16))
        h, u = _residual(h, mix, mix_post_g[l], ffn2_pre_g[l], scale=1.0, next_mode="norm")

        h, hb = _ffn(h, u, ffn2_w_gate[l], ffn2_w_up[l], ffn2_w_down[l], ffn2_post_g[l], None, "cast")

        e = _ple(p[l].reshape(m, -1).astype(BF16), hb, ple_w_proj[l].astype(BF16), ple_w_gate[l].astype(BF16))
        if l + 1 < depth:
            h, u = _residual(h, e, ple_post_g[l], ffn1_pre_g[l + 1], scale=1.0, next_mode="norm")
        else:
            h, _ = _residual(h, e, ple_post_g[l], None, scale=1.0, next_mode=None)
    return h.reshape(batch, seq, d)
```

```python
import functools

import numpy as np
import jax
import jax.numpy as jnp
from jax import lax
from jax.experimental import pallas as pl
from jax.experimental.pallas import tpu as pltpu

CHUNK = 64
N_LEFT_CHUNKS = 8
PAD = N_LEFT_CHUNKS * CHUNK
BAND = PAD + CHUNK
HEAD_DIM = 128
REL_CLIP = 128
CONV_WIDTH = 4
LRU_C = 8.0
EPS = 1e-6
NEG_INF = -1e30
LANES = 128
SUBLANES = 8
V7X_VMEM_BYTES = 64 * 1024 * 1024
VMEM_CAP_BYTES = 60000 * 1024
BF16 = jnp.bfloat16
F32 = jnp.float32


def _nbytes(shape, dtype):
    return int(np.prod(shape)) * jnp.dtype(dtype).itemsize


def _params(blocks, extra_bytes=0):
    need = 2 * sum(_nbytes(s, d) for s, d in blocks) + extra_bytes + (2 << 20)
    assert need <= VMEM_CAP_BYTES, (need, VMEM_CAP_BYTES)
    return pltpu.CompilerParams(vmem_limit_bytes=min(VMEM_CAP_BYTES, max(need, 16 << 20)))


def _rms(x, g):
    return x * lax.rsqrt(jnp.mean(x * x, axis=-1, keepdims=True) + EPS) * g


def _norm_cast_kernel(x_ref, g_ref, o_ref):
    o_ref[...] = _rms(x_ref[...], g_ref[...]).astype(o_ref.dtype)


def _norm_cast(x, g, tm=256):
    m, d = x.shape
    blocks = [((tm, d), F32), ((1, d), F32), ((tm, d), BF16)]
    return pl.pallas_call(
        _norm_cast_kernel,
        grid=(m // tm,),
        in_specs=[pl.BlockSpec((tm, d), lambda i: (i, 0)), pl.BlockSpec((1, d), lambda i: (0, 0))],
        out_specs=pl.BlockSpec((tm, d), lambda i: (i, 0)),
        out_shape=jax.ShapeDtypeStruct((m, d), BF16),
        compiler_params=_params(blocks, 2 * _nbytes((tm, d), F32)),
        name="norm_cast",
    )(x, g.reshape(1, d))


def _residual_kernel(*refs, scale, next_mode):
    if next_mode == "norm":
        h_ref, f_ref, gp_ref, gn_ref, ho_ref, uo_ref = refs
    elif next_mode == "cast":
        h_ref, f_ref, gp_ref, ho_ref, uo_ref = refs
    else:
        h_ref, f_ref, gp_ref, ho_ref = refs
    h = h_ref[...] + scale * _rms(f_ref[...], gp_ref[...])
    ho_ref[...] = h
    if next_mode == "norm":
        uo_ref[...] = _rms(h, gn_ref[...]).astype(uo_ref.dtype)
    elif next_mode == "cast":
        uo_ref[...] = h.astype(uo_ref.dtype)


def _residual(h, f, g_post, g_next, *, scale, next_mode, tm=256):
    m, d = h.shape
    row = pl.BlockSpec((tm, d), lambda i: (i, 0))
    vec = pl.BlockSpec((1, d), lambda i: (0, 0))
    args, in_specs = [h, f, g_post.reshape(1, d)], [row, row, vec]
    out_shape, out_specs = [jax.ShapeDtypeStruct((m, d), F32)], [row]
    blocks = [((tm, d), F32)] * 3
    if next_mode == "norm":
        args.append(g_next.reshape(1, d))
        in_specs.append(vec)
    if next_mode is not None:
        out_shape.append(jax.ShapeDtypeStruct((m, d), BF16))
        out_specs.append(row)
        blocks.append(((tm, d), BF16))
    out = pl.pallas_call(
        functools.partial(_residual_kernel, scale=scale, next_mode=next_mode),
        grid=(m // tm,),
        in_specs=in_specs, out_specs=out_specs, out_shape=out_shape,
        compiler_params=_params(blocks, 3 * _nbytes((tm, d), F32)),
        name="residual_" + str(next_mode),
    )(*args)
    return tuple(out) if next_mode is not None else (out[0], None)


def _mxu(a, w_ref):
    return jnp.dot(a, w_ref[...].astype(BF16), preferred_element_type=F32)


def _cast_bytes(shape, dtype):
    return 0 if dtype == BF16 else _nbytes(shape, BF16)


def _gate_up_kernel(u_ref, wg_ref, wu_ref, o_ref):
    u = u_ref[...]
    g = _mxu(u, wg_ref)
    v = _mxu(u, wu_ref)
    o_ref[...] = (jax.nn.silu(g) * v).astype(o_ref.dtype)


def _gate_up(u, wg, wu, tm=1024, tn=256):
    m, k = u.shape
    n = wg.shape[1]
    blocks = [((tm, k), BF16), ((k, tn), wg.dtype), ((k, tn), wu.dtype), ((tm, tn), BF16)]
    return pl.pallas_call(
        _gate_up_kernel,
        grid=(m // tm, n // tn),
        in_specs=[pl.BlockSpec((tm, k), lambda i, j: (i, 0)),
                  pl.BlockSpec((k, tn), lambda i, j: (0, j)),
                  pl.BlockSpec((k, tn), lambda i, j: (0, j))],
        out_specs=pl.BlockSpec((tm, tn), lambda i, j: (i, j)),
        out_shape=jax.ShapeDtypeStruct((m, n), BF16),
        compiler_params=_params(blocks, 4 * _nbytes((tm, tn), F32) + 2 * _cast_bytes((k, tn), wg.dtype)),
        name="ffn_gate_up",
    )(u, wg, wu)


def _matmul_kernel(a_ref, w_ref, o_ref):
    o_ref[...] = _mxu(a_ref[...], w_ref).astype(o_ref.dtype)


def _matmul(a, w, *, out_dtype, tm, tn, col0=0, n=None, name):
    m, k = a.shape
    n = w.shape[1] if n is None else n
    assert col0 % tn == 0 and n % tn == 0 and m % tm == 0
    jb = col0 // tn
    blocks = [((tm, k), a.dtype), ((k, tn), w.dtype), ((tm, tn), out_dtype)]
    return pl.pallas_call(
        _matmul_kernel,
        grid=(m // tm, n // tn),
        in_specs=[pl.BlockSpec((tm, k), lambda i, j: (i, 0)),
                  pl.BlockSpec((k, tn), lambda i, j: (0, j + jb))],
        out_specs=pl.BlockSpec((tm, tn), lambda i, j: (i, j)),
        out_shape=jax.ShapeDtypeStruct((m, n), out_dtype),
        compiler_params=_params(blocks, 2 * _nbytes((tm, tn), F32) + _cast_bytes((k, tn), w.dtype)),
        name=name,
    )(a, w)


def _mix_out_kernel(att_ref, rec_ref, wa_ref, wr_ref, o_ref):
    o_ref[...] = _mxu(att_ref[...], wa_ref) + _mxu(rec_ref[...], wr_ref)


def _mix_out(att, rec, w_out, tm=1024, tn=512):
    m, ka = att.shape
    kr = rec.shape[1]
    n = w_out.shape[1]
    assert ka % kr == 0
    wdt = w_out.dtype
    blocks = [((tm, ka), BF16), ((tm, kr), BF16), ((ka, tn), wdt), ((kr, tn), wdt), ((tm, tn), F32)]
    return pl.pallas_call(
        _mix_out_kernel,
        grid=(m // tm, n // tn),
        in_specs=[pl.BlockSpec((tm, ka), lambda i, j: (i, 0)),
                  pl.BlockSpec((tm, kr), lambda i, j: (i, 0)),
                  pl.BlockSpec((ka, tn), lambda i, j: (0, j)),
                  pl.BlockSpec((kr, tn), lambda i, j: (ka // kr, j))],
        out_specs=pl.BlockSpec((tm, tn), lambda i, j: (i, j)),
        out_shape=jax.ShapeDtypeStruct((m, n), F32),
        compiler_params=_params(blocks, 2 * _nbytes((tm, tn), F32) + _cast_bytes((ka + kr, tn), wdt)),
        name="mix_out",
    )(att, rec, w_out, w_out)


def _ple_kernel(p_ref, h_ref, wp_ref, wg_ref, o_ref):
    e = _mxu(p_ref[...].astype(BF16), wp_ref)
    gate = _mxu(h_ref[...], wg_ref)
    o_ref[...] = e * jax.nn.sigmoid(gate)


def _ple(p, hb, wp, wg, tm=1024, tn=512):
    m, kp = p.shape
    k = hb.shape[1]
    n = wg.shape[1]
    blocks = [((tm, kp), p.dtype), ((tm, k), BF16), ((kp, tn), wp.dtype), ((k, tn), wg.dtype), ((tm, tn), F32)]
    return pl.pallas_call(
        _ple_kernel,
        grid=(m // tm, n // tn),
        in_specs=[pl.BlockSpec((tm, kp), lambda i, j: (i, 0)),
                  pl.BlockSpec((tm, k), lambda i, j: (i, 0)),
                  pl.BlockSpec((kp, tn), lambda i, j: (0, j)),
                  pl.BlockSpec((k, tn), lambda i, j: (0, j))],
        out_specs=pl.BlockSpec((tm, tn), lambda i, j: (i, j)),
        out_shape=jax.ShapeDtypeStruct((m, n), F32),
        compiler_params=_params(blocks, 4 * _nbytes((tm, tn), F32) + _cast_bytes((k + kp, tn), wg.dtype)),
        name="ple",
    )(p, hb, wp, wg)


QBLK = 4 * CHUNK
QBAND = PAD + QBLK
RWIDTH = 1024


def _attn_kernel(q_ref, k_ref, v_ref, r_ref, o_ref, kpad_ref, vpad_ref, bias_ref):
    seq = q_ref.shape[0]
    zeros = jnp.zeros((PAD, HEAD_DIM), kpad_ref.dtype)
    kpad_ref[0:PAD, :] = zeros
    vpad_ref[0:PAD, :] = zeros
    kpad_ref[PAD:PAD + seq, :] = k_ref[...]
    vpad_ref[PAD:PAD + seq, :] = v_ref[...]

    rolled = pltpu.roll(jnp.broadcast_to(r_ref[0], (QBLK, RWIDTH)), 0, 1, stride=1, stride_axis=0)
    qrow = lax.broadcasted_iota(jnp.int32, (QBLK, QBAND), 0)
    col = lax.broadcasted_iota(jnp.int32, (QBLK, QBAND), 1)
    first = qrow & ~(CHUNK - 1)
    in_window = (col >= first) & (col < first + BAND)
    bias_ref[...] = jnp.where(in_window, rolled[:, :QBAND], NEG_INF)
    scale = HEAD_DIM ** -0.5

    def block(start, before_frame0):
        qb = q_ref[pl.ds(start, QBLK), :]
        kb = kpad_ref[pl.ds(start, QBAND), :]
        vb = vpad_ref[pl.ds(start, QBAND), :]
        s = lax.dot_general(qb, kb, (((1,), (1,)), ((), ())), preferred_element_type=F32)
        s = s * scale + bias_ref[...]
        if before_frame0:
            s = jnp.where(col >= PAD - start, s, NEG_INF)
        p = jnp.exp(s - jnp.max(s, axis=-1, keepdims=True))
        denom = jnp.sum(p, axis=-1, keepdims=True)
        o = jnp.dot(p.astype(vb.dtype), vb, preferred_element_type=F32) / denom
        o_ref[pl.ds(start, QBLK), :] = o.astype(o_ref.dtype)

    n_edge = PAD // QBLK
    for i in range(n_edge):
        block(i * QBLK, True)

    def body(i, carry):
        block(pl.multiple_of(i * QBLK, QBLK), False)
        return carry

    lax.fori_loop(n_edge, seq // QBLK, body, 0, unroll=True)


def _attention(qkv, bias_rows, batch, seq, n_heads):
    m = qkv.shape[0]
    assert seq % QBLK == 0 and PAD % QBLK == 0
    blk = (seq, HEAD_DIM)
    blocks = [(blk, BF16)] * 4 + [((1, 1, RWIDTH), F32)]
    scratch = [pltpu.VMEM((PAD + seq, HEAD_DIM), BF16), pltpu.VMEM((PAD + seq, HEAD_DIM), BF16),
               pltpu.VMEM((QBLK, QBAND), F32)]
    return pl.pallas_call(
        _attn_kernel,
        grid=(batch, n_heads),
        in_specs=[pl.BlockSpec(blk, lambda b, h: (b, h)),
                  pl.BlockSpec(blk, lambda b, h: (b, n_heads + h)),
                  pl.BlockSpec(blk, lambda b, h: (b, 2 * n_heads + h)),
                  pl.BlockSpec((1, 1, RWIDTH), lambda b, h: (h, 0, 0))],
        out_specs=pl.BlockSpec(blk, lambda b, h: (b, h)),
        out_shape=jax.ShapeDtypeStruct((m, n_heads * HEAD_DIM), BF16),
        scratch_shapes=scratch,
        compiler_params=_params(blocks, 2 * _nbytes((PAD + seq, HEAD_DIM), BF16)
                                + 2 * (seq // QBLK) * _nbytes((QBLK, QBAND), F32)),
        name="chunk_attention",
    )(qkv, qkv, qkv, bias_rows)


def _lru_kernel(xr_ref, yg_ref, cw_ref, cb_ref, wa_ref, wi_ref, ba_ref, bi_ref, lam_ref, o_ref,
                xs_ref, h_ref):
    t = pl.program_id(2)
    tt = xr_ref.shape[0]

    @pl.when(t == 0)
    def _():
        xs_ref[0:SUBLANES, :] = jnp.zeros((SUBLANES, LANES), F32)
        h_ref[...] = jnp.zeros_like(h_ref)

    @pl.when(t > 0)
    def _():
        xs_ref[0:SUBLANES, :] = xs_ref[tt:tt + SUBLANES, :]

    xs_ref[SUBLANES:SUBLANES + tt, :] = xr_ref[...]
    xc = cb_ref[...] + xs_ref[pl.ds(SUBLANES - 3, tt), :] * cw_ref[0:1, :]
    for j in range(1, CONV_WIDTH):
        xc = xc + xs_ref[pl.ds(SUBLANES - 3 + j, tt), :] * cw_ref[j:j + 1, :]

    xb = xc.astype(BF16)
    r = jax.nn.sigmoid(_mxu(xb, wa_ref.at[0]) + ba_ref[...])
    gi = jax.nn.sigmoid(_mxu(xb, wi_ref.at[0]) + bi_ref[...])
    lam = lam_ref[...]
    softplus_neg_lam = jnp.maximum(-lam, 0.0) + jnp.log1p(jnp.exp(-jnp.abs(lam)))
    log_a = -LRU_C * r * softplus_neg_lam
    a = jnp.exp(log_a)
    th = jnp.tanh(log_a)
    b = jnp.sqrt(-2.0 * th / (1.0 - th)) * (gi * xc)

    row = lax.broadcasted_iota(jnp.int32, (tt, LANES), 0) & (SUBLANES - 1)
    for s in (1, 2, 4):
        keep = row >= s
        a_prev = pltpu.roll(a, s, 0)
        b_prev = pltpu.roll(b, s, 0)
        b = jnp.where(keep, a * b_prev + b, b)
        a = jnp.where(keep, a * a_prev, a)

    gate = jax.nn.gelu(yg_ref[...])
    h = h_ref[0:1, :]
    for g in range(tt // SUBLANES):
        rows = slice(g * SUBLANES, (g + 1) * SUBLANES)
        hg = a[rows] * h + b[rows]
        o_ref[rows, :] = (hg * gate[rows]).astype(o_ref.dtype)
        h = hg[SUBLANES - 1:SUBLANES, :]
    h_ref[0:1, :] = h


def _lru_branch(xy, conv_w, conv_b, wa, wi, ba, bi, lam, batch, seq, tt=512):
    m = xy.shape[0]
    c = conv_w.shape[1]
    nb = c // LANES
    tb = seq // tt
    blk = (tt, LANES)
    vec = pl.BlockSpec((1, LANES), lambda b, n, t: (0, n))
    sq = pl.BlockSpec((1, LANES, LANES), lambda b, n, t: (n, 0, 0))
    blocks = [(blk, F32)] * 2 + [(blk, BF16)] + [((LANES, LANES), wa.dtype)] * 2
    return pl.pallas_call(
        _lru_kernel,
        grid=(batch, nb, tb),
        in_specs=[pl.BlockSpec(blk, lambda b, n, t: (b * tb + t, n)),
                  pl.BlockSpec(blk, lambda b, n, t: (b * tb + t, nb + n)),
                  pl.BlockSpec((CONV_WIDTH, LANES), lambda b, n, t: (0, n)),
                  vec, sq, sq, vec, vec, vec],
        out_specs=pl.BlockSpec(blk, lambda b, n, t: (b * tb + t, n)),
        out_shape=jax.ShapeDtypeStruct((m, c), BF16),
        scratch_shapes=[pltpu.VMEM((tt + SUBLANES, LANES), F32), pltpu.VMEM((SUBLANES, LANES), F32)],
        compiler_params=_params(blocks, 16 * _nbytes(blk, F32)),
        name="conv_rglru",
    )(xy, xy, conv_w, conv_b.reshape(1, c), wa, wi, ba.reshape(1, c), bi.reshape(1, c), lam.reshape(1, c))


def _rel_bias_rows(rel_bias):
    assert QBAND <= RWIDTH - QBLK + 1
    x = np.arange(RWIDTH)
    j_minus_q = np.where(x < QBAND, x, x - RWIDTH)
    idx = np.clip(PAD - j_minus_q, -REL_CLIP, REL_CLIP) + REL_CLIP
    return rel_bias.astype(F32)[:, idx][:, None, :]


def _ffn(h, u, w_gate, w_up, w_down, g_post, g_next, next_mode):
    act = _gate_up(u, w_gate, w_up)
    f = _matmul(act, w_down.astype(BF16), out_dtype=F32, tm=512, tn=512, name="ffn_down")
    return _residual(h, f, g_post, g_next, scale=0.5, next_mode=next_mode)


def kernel(x, p, ffn1_pre_g, ffn1_w_gate, ffn1_w_up, ffn1_w_down, ffn1_post_g, mix_pre_g, w_in, rel_bias, conv_w, conv_b, lru_w_a, lru_b_a, lru_w_i, lru_b_i, lru_lambda, w_out, mix_post_g, ffn2_pre_g, ffn2_w_gate, ffn2_w_up, ffn2_w_down, ffn2_post_g, ple_w_proj, ple_w_gate, ple_post_g):
    batch, seq, d = x.shape
    depth = p.shape[0]
    m = batch * seq
    n_heads = rel_bias.shape[1]
    att_w = n_heads * HEAD_DIM
    rnn_w = conv_w.shape[2]
    h = x.reshape(m, d)
    u = _norm_cast(h, ffn1_pre_g[0])
    for l in range(depth):
        h, u = _ffn(h, u, ffn1_w_gate[l], ffn1_w_up[l], ffn1_w_down[l], ffn1_post_g[l],
                    mix_pre_g[l], "norm")

        qkv = _matmul(u, w_in[l], out_dtype=BF16, tm=1024, tn=512, col0=0, n=3 * att_w, name="in_proj_qkv")
        xy = _matmul(u, w_in[l], out_dtype=F32, tm=1024, tn=512, col0=3 * att_w, n=2 * rnn_w,
                     name="in_proj_rnn")
        att = _attention(qkv, _rel_bias_rows(rel_bias[l]), batch, seq, n_heads)
        rec = _lru_branch(xy, conv_w[l], conv_b[l], lru_w_a[l], lru_w_i[l],
                          lru_b_a[l], lru_b_i[l], lru_lambda[l], batch, seq)
        mix = _mix_out(att, rec, w_out[l])
        h, u = _residual(h, mix, mix_post_g[l], ffn2_pre_g[l], scale=1.0, next_mode="norm")

        h, hb = _ffn(h, u, ffn2_w_gate[l], ffn2_w_up[l], ffn2_w_down[l], ffn2_post_g[l], None, "cast")

        e = _ple(p[l].reshape(m, -1), hb, ple_w_proj[l], ple_w_gate[l])
        if l + 1 < depth:
            h, u = _residual(h, e, ple_post_g[l], ffn1_pre_g[l + 1], scale=1.0, next_mode="norm")
        else:
            h, _ = _residual(h, e, ple_post_g[l], None, scale=1.0, next_mode=None)
    return h.reshape(batch, seq, d)
```

```python
import functools

import numpy as np
import jax
import jax.numpy as jnp
from jax import lax
from jax.experimental import pallas as pl
from jax.experimental.pallas import tpu as pltpu

CHUNK = 64
N_LEFT_CHUNKS = 8
PAD = N_LEFT_CHUNKS * CHUNK
BAND = PAD + CHUNK
HEAD_DIM = 128
REL_CLIP = 128
CONV_WIDTH = 4
LRU_C = 8.0
EPS = 1e-6
NEG_INF = -1e30
LANES = 128
SUBLANES = 8
V7X_VMEM_BYTES = 64 * 1024 * 1024
VMEM_CAP_BYTES = 60000 * 1024
BF16 = jnp.bfloat16
F32 = jnp.float32


def _nbytes(shape, dtype):
    return int(np.prod(shape)) * jnp.dtype(dtype).itemsize


def _params(blocks, extra_bytes=0, single_blocks=()):
    need = (2 * sum(_nbytes(s, d) for s, d in blocks) + sum(_nbytes(s, d) for s, d in single_blocks)
            + extra_bytes + (2 << 20))
    assert need <= VMEM_CAP_BYTES, (need, VMEM_CAP_BYTES)
    return pltpu.CompilerParams(vmem_limit_bytes=min(VMEM_CAP_BYTES, max(need, 16 << 20)))


def _rms(x, g):
    return x * lax.rsqrt(jnp.mean(x * x, axis=-1, keepdims=True) + EPS) * g


def _norm_cast_kernel(x_ref, g_ref, o_ref):
    o_ref[...] = _rms(x_ref[...], g_ref[...]).astype(o_ref.dtype)


def _norm_cast(x, g, tm=256):
    m, d = x.shape
    blocks = [((tm, d), F32), ((1, d), F32), ((tm, d), BF16)]
    return pl.pallas_call(
        _norm_cast_kernel,
        grid=(m // tm,),
        in_specs=[pl.BlockSpec((tm, d), lambda i: (i, 0)), pl.BlockSpec((1, d), lambda i: (0, 0))],
        out_specs=pl.BlockSpec((tm, d), lambda i: (i, 0)),
        out_shape=jax.ShapeDtypeStruct((m, d), BF16),
        compiler_params=_params(blocks, 2 * _nbytes((tm, d), F32)),
        name="norm_cast",
    )(x, g.reshape(1, d))


def _residual_kernel(*refs, scale, next_mode):
    if next_mode == "norm":
        h_ref, f_ref, gp_ref, gn_ref, ho_ref, uo_ref = refs
    elif next_mode == "cast":
        h_ref, f_ref, gp_ref, ho_ref, uo_ref = refs
    else:
        h_ref, f_ref, gp_ref, ho_ref = refs
    h = h_ref[...] + scale * _rms(f_ref[...].astype(F32), gp_ref[...])
    ho_ref[...] = h
    if next_mode == "norm":
        uo_ref[...] = _rms(h, gn_ref[...]).astype(uo_ref.dtype)
    elif next_mode == "cast":
        uo_ref[...] = h.astype(uo_ref.dtype)


def _residual(h, f, g_post, g_next, *, scale, next_mode, tm=256):
    m, d = h.shape
    row = pl.BlockSpec((tm, d), lambda i: (i, 0))
    vec = pl.BlockSpec((1, d), lambda i: (0, 0))
    args, in_specs = [h, f, g_post.reshape(1, d)], [row, row, vec]
    out_shape, out_specs = [jax.ShapeDtypeStruct((m, d), F32)], [row]
    blocks = [((tm, d), F32)] * 2 + [((tm, d), f.dtype)]
    if next_mode == "norm":
        args.append(g_next.reshape(1, d))
        in_specs.append(vec)
    if next_mode is not None:
        out_shape.append(jax.ShapeDtypeStruct((m, d), BF16))
        out_specs.append(row)
        blocks.append(((tm, d), BF16))
    out = pl.pallas_call(
        functools.partial(_residual_kernel, scale=scale, next_mode=next_mode),
        grid=(m // tm,),
        in_specs=in_specs, out_specs=out_specs, out_shape=out_shape,
        compiler_params=_params(blocks, 3 * _nbytes((tm, d), F32)),
        name="residual_" + str(next_mode),
    )(*args)
    return tuple(out) if next_mode is not None else (out[0], None)


def _mxu(a, w_ref):
    return jnp.dot(a, w_ref[...].astype(BF16), preferred_element_type=F32)


def _cast_bytes(shape, dtype):
    return 0 if dtype == BF16 else _nbytes(shape, BF16)


def _gate_up_kernel(u_ref, wg_ref, wu_ref, wd_ref, o_ref, wdb_ref):
    u = u_ref[...]
    g = _mxu(u, wg_ref)
    v = _mxu(u, wu_ref)
    o_ref[...] = (jax.nn.silu(g) * v).astype(o_ref.dtype)
    wdb_ref[...] = wd_ref[...].astype(wdb_ref.dtype)


def _gate_up(u, wg, wu, wd, tm=2048, tn=256):
    m, k = u.shape
    n = wg.shape[1]
    steps = (m // tm) * (n // tn)
    kd, nd = wd.shape
    slab = kd // steps
    assert m % tm == 0 and n % tn == 0 and kd % steps == 0 and slab % (2 * SUBLANES) == 0
    nj = n // tn
    blocks = [((k, tn), wg.dtype), ((k, tn), wu.dtype), ((tm, tn), BF16), ((slab, nd), wd.dtype), ((slab, nd), BF16)]
    return pl.pallas_call(
        _gate_up_kernel,
        grid=(m // tm, nj),
        in_specs=[pl.BlockSpec((tm, k), lambda i, j: (i, 0), pipeline_mode=pl.Buffered(1)),
                  pl.BlockSpec((k, tn), lambda i, j: (0, j)),
                  pl.BlockSpec((k, tn), lambda i, j: (0, j)),
                  pl.BlockSpec((slab, nd), lambda i, j: (i * nj + j, 0))],
        out_specs=[pl.BlockSpec((tm, tn), lambda i, j: (i, j)),
                   pl.BlockSpec((slab, nd), lambda i, j: (i * nj + j, 0))],
        out_shape=[jax.ShapeDtypeStruct((m, n), BF16), jax.ShapeDtypeStruct((kd, nd), BF16)],
        compiler_params=_params(blocks, 6 * _nbytes((tm, tn), F32) + 2 * _cast_bytes((k, tn), wg.dtype),
                                single_blocks=[((tm, k), BF16)]),
        name="ffn_gate_up",
    )(u, wg, wu, wd)


def _matmul_kernel(a_ref, w_ref, o_ref):
    o_ref[...] = _mxu(a_ref[...], w_ref).astype(o_ref.dtype)


def _matmul(a, w, *, out_dtype, tm, tn, col0=0, n=None, name):
    m, k = a.shape
    n = w.shape[1] if n is None else n
    assert col0 % tn == 0 and n % tn == 0 and m % tm == 0
    jb = col0 // tn
    blocks = [((tm, k), a.dtype), ((k, tn), w.dtype), ((tm, tn), out_dtype)]
    return pl.pallas_call(
        _matmul_kernel,
        grid=(m // tm, n // tn),
        in_specs=[pl.BlockSpec((tm, k), lambda i, j: (i, 0)),
                  pl.BlockSpec((k, tn), lambda i, j: (0, j + jb))],
        out_specs=pl.BlockSpec((tm, tn), lambda i, j: (i, j)),
        out_shape=jax.ShapeDtypeStruct((m, n), out_dtype),
        compiler_params=_params(blocks, 2 * _nbytes((tm, tn), F32) + _cast_bytes((k, tn), w.dtype)),
        name=name,
    )(a, w)


def _mix_out_kernel(att_ref, rec_ref, wa_ref, wr_ref, o_ref):
    o_ref[...] = (_mxu(att_ref[...], wa_ref) + _mxu(rec_ref[...], wr_ref)).astype(o_ref.dtype)


def _mix_out(att, rec, w_out, tm=1024, tn=512):
    m, ka = att.shape
    kr = rec.shape[1]
    n = w_out.shape[1]
    assert ka % kr == 0
    wdt = w_out.dtype
    blocks = [((tm, ka), BF16), ((tm, kr), BF16), ((ka, tn), wdt), ((kr, tn), wdt), ((tm, tn), BF16)]
    return pl.pallas_call(
        _mix_out_kernel,
        grid=(m // tm, n // tn),
        in_specs=[pl.BlockSpec((tm, ka), lambda i, j: (i, 0)),
                  pl.BlockSpec((tm, kr), lambda i, j: (i, 0)),
                  pl.BlockSpec((ka, tn), lambda i, j: (0, j)),
                  pl.BlockSpec((kr, tn), lambda i, j: (ka // kr, j))],
        out_specs=pl.BlockSpec((tm, tn), lambda i, j: (i, j)),
        out_shape=jax.ShapeDtypeStruct((m, n), BF16),
        compiler_params=_params(blocks, 2 * _nbytes((tm, tn), F32) + _cast_bytes((ka + kr, tn), wdt)),
        name="mix_out",
    )(att, rec, w_out, w_out)


def _ple_kernel(p_ref, h_ref, wp_ref, wg_ref, o_ref):
    e = _mxu(p_ref[...].astype(BF16), wp_ref)
    gate = _mxu(h_ref[...], wg_ref)
    o_ref[...] = (e * jax.nn.sigmoid(gate)).astype(o_ref.dtype)


def _ple(p, hb, wp, wg, tm=1024, tn=512):
    m, kp = p.shape
    k = hb.shape[1]
    n = wg.shape[1]
    blocks = [((tm, kp), p.dtype), ((tm, k), BF16), ((kp, tn), wp.dtype), ((k, tn), wg.dtype), ((tm, tn), BF16)]
    return pl.pallas_call(
        _ple_kernel,
        grid=(m // tm, n // tn),
        in_specs=[pl.BlockSpec((tm, kp), lambda i, j: (i, 0)),
                  pl.BlockSpec((tm, k), lambda i, j: (i, 0)),
                  pl.BlockSpec((kp, tn), lambda i, j: (0, j)),
                  pl.BlockSpec((k, tn), lambda i, j: (0, j))],
        out_specs=pl.BlockSpec((tm, tn), lambda i, j: (i, j)),
        out_shape=jax.ShapeDtypeStruct((m, n), BF16),
        compiler_params=_params(blocks, 4 * _nbytes((tm, tn), F32) + _cast_bytes((k + kp, tn), wg.dtype)),
        name="ple",
    )(p, hb, wp, wg)


QBLK = 4 * CHUNK
QBAND = PAD + QBLK
RWIDTH = 1024


def _attn_kernel(q_ref, k_ref, v_ref, r_ref, o_ref, kpad_ref, vpad_ref, bias_ref):
    seq = q_ref.shape[0]
    zeros = jnp.zeros((PAD, HEAD_DIM), kpad_ref.dtype)
    kpad_ref[0:PAD, :] = zeros
    vpad_ref[0:PAD, :] = zeros
    kpad_ref[PAD:PAD + seq, :] = k_ref[...]
    vpad_ref[PAD:PAD + seq, :] = v_ref[...]

    rolled = pltpu.roll(jnp.broadcast_to(r_ref[0], (QBLK, RWIDTH)), 0, 1, stride=1, stride_axis=0)
    qrow = lax.broadcasted_iota(jnp.int32, (QBLK, QBAND), 0)
    col = lax.broadcasted_iota(jnp.int32, (QBLK, QBAND), 1)
    first = qrow & ~(CHUNK - 1)
    in_window = (col >= first) & (col < first + BAND)
    bias_ref[...] = jnp.where(in_window, rolled[:, :QBAND], NEG_INF)
    scale = HEAD_DIM ** -0.5

    def block(start, before_frame0):
        qb = q_ref[pl.ds(start, QBLK), :]
        kb = kpad_ref[pl.ds(start, QBAND), :]
        vb = vpad_ref[pl.ds(start, QBAND), :]
        s = lax.dot_general(qb, kb, (((1,), (1,)), ((), ())), preferred_element_type=F32)
        s = s * scale + bias_ref[...]
        if before_frame0:
            s = jnp.where(col >= PAD - start, s, NEG_INF)
        p = jnp.exp(s - jnp.max(s, axis=-1, keepdims=True))
        denom = jnp.sum(p, axis=-1, keepdims=True)
        o = jnp.dot(p.astype(vb.dtype), vb, preferred_element_type=F32) / denom
        o_ref[pl.ds(start, QBLK), :] = o.astype(o_ref.dtype)

    n_edge = PAD // QBLK
    for i in range(n_edge):
        block(i * QBLK, True)

    def body(i, carry):
        block(pl.multiple_of(i * QBLK, QBLK), False)
        return carry

    lax.fori_loop(n_edge, seq // QBLK, body, 0, unroll=True)


def _attention(qkv, bias_rows, batch, seq, n_heads):
    m = qkv.shape[0]
    assert seq % QBLK == 0 and PAD % QBLK == 0
    blk = (seq, HEAD_DIM)
    blocks = [(blk, BF16)] * 4 + [((1, 1, RWIDTH), F32)]
    scratch = [pltpu.VMEM((PAD + seq, HEAD_DIM), BF16), pltpu.VMEM((PAD + seq, HEAD_DIM), BF16),
               pltpu.VMEM((QBLK, QBAND), F32)]
    return pl.pallas_call(
        _attn_kernel,
        grid=(batch, n_heads),
        in_specs=[pl.BlockSpec(blk, lambda b, h: (b, h)),
                  pl.BlockSpec(blk, lambda b, h: (b, n_heads + h)),
                  pl.BlockSpec(blk, lambda b, h: (b, 2 * n_heads + h)),
                  pl.BlockSpec((1, 1, RWIDTH), lambda b, h: (h, 0, 0))],
        out_specs=pl.BlockSpec(blk, lambda b, h: (b, h)),
        out_shape=jax.ShapeDtypeStruct((m, n_heads * HEAD_DIM), BF16),
        scratch_shapes=scratch,
        compiler_params=_params(blocks, 2 * _nbytes((PAD + seq, HEAD_DIM), BF16)
                                + 2 * (seq // QBLK) * _nbytes((QBLK, QBAND), F32)),
        name="chunk_attention",
    )(qkv, qkv, qkv, bias_rows)


def _lru_kernel(xr_ref, yg_ref, cw_ref, cb_ref, wa_ref, wi_ref, ba_ref, bi_ref, lam_ref, o_ref,
                xs_ref, h_ref):
    t = pl.program_id(2)
    tt, width = xr_ref.shape

    @pl.when(t == 0)
    def _():
        xs_ref[0:SUBLANES, :] = jnp.zeros((SUBLANES, width), F32)
        h_ref[...] = jnp.zeros_like(h_ref)

    @pl.when(t > 0)
    def _():
        xs_ref[0:SUBLANES, :] = xs_ref[tt:tt + SUBLANES, :]

    xs_ref[SUBLANES:SUBLANES + tt, :] = xr_ref[...]
    row = lax.broadcasted_iota(jnp.int32, (tt, LANES), 0) & (SUBLANES - 1)

    for c in range(width // LANES):
        cols = slice(c * LANES, (c + 1) * LANES)
        xc = cb_ref[:, cols] + xs_ref[pl.ds(SUBLANES - 3, tt), cols] * cw_ref[0:1, cols]
        for j in range(1, CONV_WIDTH):
            xc = xc + xs_ref[pl.ds(SUBLANES - 3 + j, tt), cols] * cw_ref[j:j + 1, cols]

        xb = xc.astype(BF16)
        r = jax.nn.sigmoid(_mxu(xb, wa_ref.at[c]) + ba_ref[:, cols])
        gi = jax.nn.sigmoid(_mxu(xb, wi_ref.at[c]) + bi_ref[:, cols])
        lam = lam_ref[:, cols]
        softplus_neg_lam = jnp.maximum(-lam, 0.0) + jnp.log1p(jnp.exp(-jnp.abs(lam)))
        log_a = -LRU_C * r * softplus_neg_lam
        a = jnp.exp(log_a)
        th = jnp.tanh(log_a)
        b = jnp.sqrt(-2.0 * th / (1.0 - th)) * (gi * xc)

        for s in (1, 2, 4):
            keep = row >= s
            a_prev = pltpu.roll(a, s, 0)
            b_prev = pltpu.roll(b, s, 0)
            b = jnp.where(keep, a * b_prev + b, b)
            a = jnp.where(keep, a * a_prev, a)

        gate = jax.nn.gelu(yg_ref[:, cols])
        h = h_ref[0:1, cols]
        for g in range(tt // SUBLANES):
            rows = slice(g * SUBLANES, (g + 1) * SUBLANES)
            hg = a[rows] * h + b[rows]
            o_ref[rows, cols] = (hg * gate[rows]).astype(o_ref.dtype)
            h = hg[SUBLANES - 1:SUBLANES, :]
        h_ref[0:1, cols] = h


def _lru_branch(xy, conv_w, conv_b, wa, wi, ba, bi, lam, batch, seq, tt=512, nc=2):
    m = xy.shape[0]
    c = conv_w.shape[1]
    width = nc * LANES
    nb = c // width
    tb = seq // tt
    blk = (tt, width)
    vec = pl.BlockSpec((1, width), lambda b, n, t: (0, n))
    sq = pl.BlockSpec((nc, LANES, LANES), lambda b, n, t: (n, 0, 0))
    blocks = [(blk, F32)] * 2 + [(blk, BF16)] + [((nc, LANES, LANES), wa.dtype)] * 2
    return pl.pallas_call(
        _lru_kernel,
        grid=(batch, nb, tb),
        in_specs=[pl.BlockSpec(blk, lambda b, n, t: (b * tb + t, n)),
                  pl.BlockSpec(blk, lambda b, n, t: (b * tb + t, nb + n)),
                  pl.BlockSpec((CONV_WIDTH, width), lambda b, n, t: (0, n)),
                  vec, sq, sq, vec, vec, vec],
        out_specs=pl.BlockSpec(blk, lambda b, n, t: (b * tb + t, n)),
        out_shape=jax.ShapeDtypeStruct((m, c), BF16),
        scratch_shapes=[pltpu.VMEM((tt + SUBLANES, width), F32), pltpu.VMEM((SUBLANES, width), F32)],
        compiler_params=_params(blocks, 16 * _nbytes(blk, F32)),
        name="conv_rglru",
    )(xy, xy, conv_w, conv_b.reshape(1, c), wa, wi, ba.reshape(1, c), bi.reshape(1, c), lam.reshape(1, c))


def _rel_bias_rows(rel_bias):
    assert QBAND <= RWIDTH - QBLK + 1
    x = np.arange(RWIDTH)
    j_minus_q = np.where(x < QBAND, x, x - RWIDTH)
    idx = np.clip(PAD - j_minus_q, -REL_CLIP, REL_CLIP) + REL_CLIP
    return rel_bias.astype(F32)[:, idx][:, None, :]


def _ffn(h, u, w_gate, w_up, w_down, g_post, g_next, next_mode):
    act, w_down_b = _gate_up(u, w_gate, w_up, w_down)
    f = _matmul(act, w_down_b, out_dtype=BF16, tm=512, tn=512, name="ffn_down")
    return _residual(h, f, g_post, g_next, scale=0.5, next_mode=next_mode)


def kernel(x, p, ffn1_pre_g, ffn1_w_gate, ffn1_w_up, ffn1_w_down, ffn1_post_g, mix_pre_g, w_in, rel_bias, conv_w, conv_b, lru_w_a, lru_b_a, lru_w_i, lru_b_i, lru_lambda, w_out, mix_post_g, ffn2_pre_g, ffn2_w_gate, ffn2_w_up, ffn2_w_down, ffn2_post_g, ple_w_proj, ple_w_gate, ple_post_g):
    batch, seq, d = x.shape
    depth = p.shape[0]
    m = batch * seq
    n_heads = rel_bias.shape[1]
    att_w = n_heads * HEAD_DIM
    rnn_w = conv_w.shape[2]
    h = x.reshape(m, d)
    u = _norm_cast(h, ffn1_pre_g[0])
    for l in range(depth):
        h, u = _ffn(h, u, ffn1_w_gate[l], ffn1_w_up[l], ffn1_w_down[l], ffn1_post_g[l],
                    mix_pre_g[l], "norm")

        qkv = _matmul(u, w_in[l], out_dtype=BF16, tm=1024, tn=512, col0=0, n=3 * att_w, name="in_proj_qkv")
        xy = _matmul(u, w_in[l], out_dtype=F32, tm=1024, tn=512, col0=3 * att_w, n=2 * rnn_w,
                     name="in_proj_rnn")
        att = _attention(qkv, _rel_bias_rows(rel_bias[l]), batch, seq, n_heads)
        rec = _lru_branch(xy, conv_w[l], conv_b[l], lru_w_a[l], lru_w_i[l],
                          lru_b_a[l], lru_b_i[l], lru_lambda[l], batch, seq)
        mix = _mix_out(att, rec, w_out[l])
        h, u = _residual(h, mix, mix_post_g[l], ffn2_pre_g[l], scale=1.0, next_mode="norm")

        h, hb = _ffn(h, u, ffn2_w_gate[l], ffn2_w_up[l], ffn2_w_down[l], ffn2_post_g[l], None, "cast")

        e = _ple(p[l].reshape(m, -1), hb, ple_w_proj[l], ple_w_gate[l])
        if l + 1 < depth:
            h, u = _residual(h, e, ple_post_g[l], ffn1_pre_g[l + 1], scale=1.0, next_mode="norm")
        else:
            h, _ = _residual(h, e, ple_post_g[l], None, scale=1.0, next_mode=None)
    return h.reshape(batch, seq, d)
```
